```python
import jax, jax.numpy as jnp
from jax import lax
import numpy as np

D_MODEL = 1024
BATCH = 8
SEQ = 4096
DEPTH = 2

POOL_GROUPS = 4
POOL_WINDOWS = (2, 4, 8, 16)
POOL_WIDTH = 3 * D_MODEL // 8
POOL_GROUP_DIM = POOL_WIDTH // POOL_GROUPS
CONV_WIDTH = 3 * D_MODEL // 8
HEAD_DIM = 64
DIL_CONFIGS = ((128, 1), (512, 4), (2048, 16))
HEADS_PER_GROUP = 4
ATTN_HEADS = HEADS_PER_GROUP * len(DIL_CONFIGS)
ATTN_WIDTH = ATTN_HEADS * HEAD_DIM
ATTN_OUT_WIDTH = HEADS_PER_GROUP * HEAD_DIM
ROT_DIM = HEAD_DIM // 4
ROPE_THETA = 500000.0
QUERY_BLOCK = 128
N_BRANCHES = 3
IN_SPLITS = (POOL_WIDTH, CONV_WIDTH, CONV_WIDTH, CONV_WIDTH, ATTN_WIDTH, ATTN_WIDTH, ATTN_WIDTH, N_BRANCHES * D_MODEL)
IN_WIDTH = POOL_WIDTH + 3 * CONV_WIDTH + 3 * ATTN_WIDTH + N_BRANCHES * D_MODEL
MEM_LEN = 256
MEM_HEADS = 4
MEM_HEAD_DIM = D_MODEL // 8
MEM_WIDTH = MEM_HEADS * MEM_HEAD_DIM
D_FF = ((8 * D_MODEL // 3 + 127) // 128) * 128
CONV_K = 3
RMS_EPS = 1e-6

kernel_name = "hybrid_gated_pool_conv_dilattn_block"


def rmsnorm(x, g):
    xf = x.astype(jnp.float32)
    y = xf * lax.rsqrt(jnp.mean(xf * xf, axis=-1, keepdims=True) + RMS_EPS)
    return (y * g.astype(jnp.float32)).astype(x.dtype)


def split_cols(u, sizes):
    outs, start = [], 0
    for s in sizes:
        outs.append(u[..., start:start + s])
        start += s
    return outs


def causal_dwconv(u, w):
    k, c = w.shape
    return lax.conv_general_dilated(
        u, w[:, None, :].astype(u.dtype), window_strides=(1,), padding=((k - 1, 0),),
        dimension_numbers=("NWC", "WIO", "NWC"), feature_group_count=c)


def rope_tables(positions):
    inv = ROPE_THETA ** (-jnp.arange(0, ROT_DIM, 2, dtype=jnp.float32) / ROT_DIM)
    ang = positions.astype(jnp.float32)[..., None] * inv
    return jnp.cos(ang)[:, :, None, :], jnp.sin(ang)[:, :, None, :]


def apply_partial_rope(u, cos, sin):
    half = ROT_DIM // 2
    uf = u[..., :ROT_DIM].astype(jnp.float32)
    u1, u2 = uf[..., :half], uf[..., half:]
    rot = jnp.concatenate([u1 * cos - u2 * sin, u2 * cos + u1 * sin], axis=-1).astype(u.dtype)
    return jnp.concatenate([rot, u[..., ROT_DIM:]], axis=-1)


def multiscale_pool(u, pool_w, pool_scale):
    b, s, _ = u.shape
    ug = u.reshape(b, s, POOL_GROUPS, POOL_GROUP_DIM).astype(jnp.float32)
    cs = jnp.cumsum(ug, axis=1)
    t = jnp.arange(s)
    outs = []
    for g, w in enumerate(POOL_WINDOWS):
        c = cs[:, :, g]
        prev = jnp.pad(c, ((0, 0), (w, 0), (0, 0)))[:, :s]
        cnt = jnp.minimum(t + 1, w).astype(jnp.float32)[None, :, None]
        outs.append((c - prev) / cnt - ug[:, :, g])
    pooled = jnp.stack(outs, axis=2)
    mixed = jnp.einsum("bsgc,gcd->bsgd", pooled, pool_w.astype(jnp.float32))
    return (mixed.reshape(b, s, POOL_WIDTH) * pool_scale.astype(jnp.float32)).astype(u.dtype)


def dilated_attention(q, k, v):
    b, s, _, hd = q.shape
    ng = len(DIL_CONFIGS)
    def grp(u):
        return u.reshape(b, s, ng, HEADS_PER_GROUP, hd).transpose(2, 0, 3, 1, 4)
    qg, kg, vg = grp(q), grp(k), grp(v)
    scale = hd ** -0.5

    def block(bi):
        start = bi * QUERY_BLOCK
        t = start + jnp.arange(QUERY_BLOCK)
        qb = lax.dynamic_slice_in_dim(qg, start, QUERY_BLOCK, axis=3)
        outs, lses = [], []
        for g, (window, dil) in enumerate(DIL_CONFIGS):
            offs = jnp.arange(window // dil + 1) * dil
            idx = t[:, None] - offs[None, :]
            valid = idx >= 0
            idx = jnp.maximum(idx, 0)
            kk = jnp.take(kg[g], idx, axis=2)
            vv = jnp.take(vg[g], idx, axis=2)
            sc = jnp.einsum("bhqd,bhqkd->bhqk", qb[g], kk).astype(jnp.float32) * scale
            sc = jnp.where(valid, sc, -jnp.inf)
            m = jnp.max(sc, axis=-1, keepdims=True)
            p = jnp.exp(sc - m)
            den = jnp.sum(p, axis=-1, keepdims=True)
            o = jnp.einsum("bhqk,bhqkd->bhqd", p, vv.astype(jnp.float32)) / den
            outs.append(o)
            lses.append(m + jnp.log(den))
        wts = jax.nn.softmax(jnp.stack(lses, axis=0), axis=0)
        return jnp.sum(wts * jnp.stack(outs, axis=0), axis=0).astype(q.dtype)

    out = lax.map(block, jnp.arange(s // QUERY_BLOCK))
    return out.transpose(1, 0, 3, 2, 4).reshape(b, s, ATTN_OUT_WIDTH)


def memory_cross_attention(h, mem_n, w_q, w_kv, w_o):
    b, s, _ = h.shape
    q = (h @ w_q).reshape(b, s, MEM_HEADS, MEM_HEAD_DIM)
    kv = mem_n @ w_kv
    k = kv[..., :MEM_WIDTH].reshape(b, -1, MEM_HEADS, MEM_HEAD_DIM)
    v = kv[..., MEM_WIDTH:].reshape(b, -1, MEM_HEADS, MEM_HEAD_DIM)
    sc = jnp.einsum("bshd,bmhd->bhsm", q, k).astype(jnp.float32) * (MEM_HEAD_DIM ** -0.5)
    p = jax.nn.softmax(sc, axis=-1)
    o = jnp.einsum("bhsm,bmhd->bshd", p, v.astype(jnp.float32)).astype(h.dtype)
    return o.reshape(b, s, MEM_WIDTH) @ w_o


def setup_inputs(seed: int = 0) -> dict:
    key = jax.random.key(seed)
    ks = jax.random.split(key, 32)
    f32 = jnp.float32

    def nrm(k, shape, fan_in):
        return jax.random.normal(k, shape, f32) * (fan_in ** -0.5)

    def gain(k, n):
        return 1.0 + 0.05 * jax.random.normal(k, (DEPTH, n), f32)

    return {
        "x": jax.random.normal(ks[0], (BATCH, SEQ, D_MODEL), f32),
        "mem": jax.random.normal(ks[1], (BATCH, MEM_LEN, D_MODEL), f32),
        "positions": jnp.broadcast_to(jnp.arange(SEQ, dtype=jnp.int32), (BATCH, SEQ)),
        "norm_mix_pre": gain(ks[2], D_MODEL),
        "norm_mix_post": gain(ks[3], D_MODEL),
        "w_in": nrm(ks[4], (DEPTH, D_MODEL, IN_WIDTH), D_MODEL),
        "pool_w": nrm(ks[5], (DEPTH, POOL_GROUPS, POOL_GROUP_DIM, POOL_GROUP_DIM), POOL_GROUP_DIM),
        "pool_scale": 1.0 + 0.1 * jax.random.normal(ks[6], (DEPTH, POOL_WIDTH), f32),
        "conv_b_w": nrm(ks[7], (DEPTH, CONV_K, CONV_WIDTH), CONV_K),
        "w_branch_a": nrm(ks[8], (DEPTH, POOL_WIDTH, D_MODEL), POOL_WIDTH),
        "w_branch_b": nrm(ks[9], (DEPTH, CONV_WIDTH, D_MODEL), CONV_WIDTH),
        "w_branch_c": nrm(ks[10], (DEPTH, ATTN_OUT_WIDTH, D_MODEL), ATTN_OUT_WIDTH),
        "w_out": nrm(ks[11], (DEPTH, D_MODEL, D_MODEL), D_MODEL),
        "norm_mem_pre": gain(ks[12], D_MODEL),
        "norm_mem_post": gain(ks[13], D_MODEL),
        "norm_memkv": gain(ks[14], D_MODEL),
        "w_mq": nrm(ks[15], (DEPTH, D_MODEL, MEM_WIDTH), D_MODEL),
        "w_mkv": nrm(ks[16], (DEPTH, D_MODEL, 2 * MEM_WIDTH), D_MODEL),
        "w_mo": nrm(ks[17], (DEPTH, MEM_WIDTH, D_MODEL), MEM_WIDTH),
        "norm_ffn_pre": gain(ks[18], D_MODEL),
        "norm_ffn_post": gain(ks[19], D_MODEL),
        "w_up": nrm(ks[20], (DEPTH, D_MODEL, 2 * D_FF), D_MODEL),
        "conv_ffn_w": nrm(ks[21], (DEPTH, CONV_K, D_FF), CONV_K),
        "w_down": nrm(ks[22], (DEPTH, D_FF, D_MODEL), D_FF),
    }


def reference(x, mem, positions, norm_mix_pre, norm_mix_post, w_in, pool_w, pool_scale, conv_b_w,
              w_branch_a, w_branch_b, w_branch_c, w_out, norm_mem_pre, norm_mem_post, norm_memkv,
              w_mq, w_mkv, w_mo, norm_ffn_pre, norm_ffn_post, w_up, conv_ffn_w, w_down):
    b, s, d = x.shape
    cos, sin = rope_tables(positions)
    cos, sin = cos.astype(x.dtype), sin.astype(x.dtype)
    for l in range(DEPTH):
        h = rmsnorm(x, norm_mix_pre[l])
        a_in, b_x, b_b, b_c, q, k, v, gate_in = split_cols(h @ w_in[l], IN_SPLITS)
        br_a = multiscale_pool(a_in, pool_w[l], pool_scale[l]) @ w_branch_a[l]
        br_b = (b_b * causal_dwconv(b_c * b_x, conv_b_w[l])) @ w_branch_b[l]
        q = apply_partial_rope(q.reshape(b, s, ATTN_HEADS, HEAD_DIM), cos, sin)
        k = apply_partial_rope(k.reshape(b, s, ATTN_HEADS, HEAD_DIM), cos, sin)
        v = v.reshape(b, s, ATTN_HEADS, HEAD_DIM)
        br_c = dilated_attention(q, k, v) @ w_branch_c[l]
        gates = jax.nn.sigmoid(gate_in.astype(jnp.float32)).astype(x.dtype).reshape(b, s, N_BRANCHES, d)
        merged = gates[:, :, 0] * br_a + gates[:, :, 1] * br_b + gates[:, :, 2] * br_c
        x = x + rmsnorm(merged @ w_out[l], norm_mix_post[l])
        h = rmsnorm(x, norm_mem_pre[l])
        mem_n = rmsnorm(mem, norm_memkv[l])
        x = x + rmsnorm(memory_cross_attention(h, mem_n, w_mq[l], w_mkv[l], w_mo[l]), norm_mem_post[l])
        h = rmsnorm(x, norm_ffn_pre[l])
        u = h @ w_up[l]
        ua, ub = u[..., :D_FF], u[..., D_FF:]
        y = (jax.nn.silu(causal_dwconv(ua, conv_ffn_w[l])) * ub) @ w_down[l]
        x = x + rmsnorm(y, norm_ffn_post[l])
    return x
```

```python
import functools

import jax
import jax.numpy as jnp
from jax import lax
from jax.experimental import pallas as pl
from jax.experimental.pallas import tpu as pltpu

F32 = jnp.float32
BF16 = jnp.bfloat16

D_MODEL = 1024
POOL_GROUPS = 4
POOL_WINDOWS = (2, 4, 8, 16)
POOL_WIDTH = 384
POOL_GROUP_DIM = POOL_WIDTH // POOL_GROUPS
POOL_HALO = 16
CONV_WIDTH = 384
CONV_HALO = 8
HEAD_DIM = 64
DILATIONS = (1, 4, 16)
WINDOW_KEYS = 128
HEADS_PER_GROUP = 4
GROUP_WIDTH = HEADS_PER_GROUP * HEAD_DIM
ATTN_WIDTH = len(DILATIONS) * GROUP_WIDTH
ROT_DIM = 16
ROT_HALF = ROT_DIM // 2
ROPE_THETA = 500000.0
MEM_HEADS = 4
MEM_HEAD_DIM = 128
MEM_WIDTH = MEM_HEADS * MEM_HEAD_DIM
D_FF = 2816
FF_CHUNK = 256
RMS_EPS = 1e-6
NEG_BIG = -1e30

OFF_A = 0
OFF_BX = OFF_A + POOL_WIDTH
OFF_BB = OFF_BX + CONV_WIDTH
OFF_BC = OFF_BB + CONV_WIDTH
OFF_Q = OFF_BC + CONV_WIDTH
OFF_K = OFF_Q + ATTN_WIDTH
OFF_V = OFF_K + ATTN_WIDTH
OFF_G = OFF_V + ATTN_WIDTH
IN_WIDTH = OFF_G + 3 * D_MODEL

LANES = 128
ROW_TILE = 512
ATTN_CHUNK = 2048
Q_BLOCK = 128
VMEM_LIMIT = 56 * 1024 * 1024


def _rms(x, g):
    ms = jnp.mean(x * x, axis=-1, keepdims=True)
    return x * lax.rsqrt(ms + RMS_EPS) * g


def _dot(a, b):
    return jnp.dot(a, b, preferred_element_type=F32)


def _shift_rows(cur, halo, k):
    hr = halo.shape[0]
    ext = jnp.concatenate([halo, cur], axis=0)
    return pltpu.roll(ext, k, axis=0)[hr:]


def _const_spec(shape):
    nd = len(shape)
    return pl.BlockSpec(shape, lambda *_: (0,) * nd, pipeline_mode=pl.Buffered(1))


def _rope_kernel(pos_ref, inv_ref, cos_ref, sin_ref, nsin_ref):
    ang = pos_ref[...].astype(F32) * inv_ref[...]
    s = jnp.sin(ang)
    cos_ref[...] = jnp.cos(ang)
    sin_ref[...] = s
    nsin_ref[...] = -s


def _rope_tables(positions):
    t = positions.size
    pos_rep = jnp.repeat(positions.reshape(t), ROT_HALF).reshape(t * ROT_HALF // LANES, LANES)
    inv = ROPE_THETA ** (-jnp.arange(0, ROT_DIM, 2, dtype=F32) / ROT_DIM)
    inv_row = jnp.tile(inv, LANES // ROT_HALF).reshape(1, LANES)
    rows = pos_rep.shape[0]
    rb = min(rows, 512)
    out = jax.ShapeDtypeStruct((rows, LANES), F32)
    cos, sin, nsin = pl.pallas_call(
        _rope_kernel,
        grid=(rows // rb,),
        in_specs=[pl.BlockSpec((rb, LANES), lambda i: (i, 0)), pl.BlockSpec((1, LANES), lambda i: (0, 0))],
        out_specs=[pl.BlockSpec((rb, LANES), lambda i: (i, 0))] * 3,
        out_shape=[out] * 3,
        name="rope_tables",
    )(pos_rep, inv_row)
    cos, sin, nsin = (u.reshape(t, ROT_HALF) for u in (cos, sin, nsin))
    one = jnp.ones((t, HEAD_DIM - ROT_DIM), F32)
    zero = jnp.zeros((t, HEAD_DIM - ROT_DIM), F32)
    z8 = jnp.zeros((t, ROT_HALF), F32)
    reps = LANES // HEAD_DIM
    c = jnp.tile(jnp.concatenate([cos, cos, one], axis=1), (1, reps))
    s1 = jnp.tile(jnp.concatenate([nsin, z8, zero], axis=1), (1, reps))
    s2 = jnp.tile(jnp.concatenate([z8, sin, zero], axis=1), (1, reps))
    return c, s1, s2


def _mix_in_kernel(x_ref, g_ref, w_in_ref, pw_ref, ps_ref, cw_ref, wa_ref, wb_ref, c_ref, s1_ref, s2_ref,
                   part_ref, g2_ref, q0_ref, q1_ref, q2_ref, k0_ref, k1_ref, k2_ref, v0_ref, v1_ref, v2_ref,
                   halo_a, halo_c):
    tm = x_ref.shape[1]
    j = pl.program_id(1)

    @pl.when(j == 0)
    def _():
        halo_a[...] = jnp.zeros_like(halo_a)
        halo_c[...] = jnp.zeros_like(halo_c)

    h = _rms(x_ref[0], g_ref[...]).astype(BF16)

    def proj(lo, width):
        return _dot(h, w_in_ref[:, lo:lo + width])

    a = proj(OFF_A, POOL_WIDTH)
    ha = halo_a[...]
    ext = jnp.concatenate([ha, a], axis=0)
    s2w = ext + pltpu.roll(ext, 1, axis=0)
    s4w = s2w + pltpu.roll(s2w, 2, axis=0)
    s8w = s4w + pltpu.roll(s4w, 4, axis=0)
    s16w = s8w + pltpu.roll(s8w, 8, axis=0)
    col = lax.broadcasted_iota(jnp.int32, (tm, POOL_WIDTH), 1)
    row = lax.broadcasted_iota(jnp.int32, (tm, POOL_WIDTH), 0) + j * tm
    g0, g1, g2 = (col < POOL_GROUP_DIM), (col < 2 * POOL_GROUP_DIM), (col < 3 * POOL_GROUP_DIM)
    wsum = jnp.where(g0, s2w[POOL_HALO:], jnp.where(g1, s4w[POOL_HALO:], jnp.where(g2, s8w[POOL_HALO:], s16w[POOL_HALO:])))
    win = jnp.where(g0, POOL_WINDOWS[0], jnp.where(g1, POOL_WINDOWS[1], jnp.where(g2, POOL_WINDOWS[2], POOL_WINDOWS[3])))
    cnt = jnp.minimum(row + 1, win).astype(F32)
    pooled = wsum / cnt - a
    halo_a[...] = a[tm - POOL_HALO:]
    mixed = _dot(pooled.astype(BF16), pw_ref[...]) * ps_ref[...]
    br_a = _dot(mixed.astype(BF16), wa_ref[...])

    cx = proj(OFF_BC, CONV_WIDTH) * proj(OFF_BX, CONV_WIDTH)
    hc = halo_c[...]
    cw = cw_ref[...]
    conv = cw[2:3] * cx + cw[1:2] * _shift_rows(cx, hc, 1) + cw[0:1] * _shift_rows(cx, hc, 2)
    halo_c[...] = cx[tm - CONV_HALO:]
    yb = proj(OFF_BB, CONV_WIDTH) * conv
    br_b = _dot(yb.astype(BF16), wb_ref[...])

    gate_a = jax.nn.sigmoid(proj(OFF_G, D_MODEL))
    gate_b = jax.nn.sigmoid(proj(OFF_G + D_MODEL, D_MODEL))
    part_ref[0] = (gate_a * br_a + gate_b * br_b).astype(BF16)
    g2_ref[0] = jax.nn.sigmoid(proj(OFF_G + 2 * D_MODEL, D_MODEL)).astype(BF16)

    reps = ATTN_WIDTH // LANES
    ct = jnp.tile(c_ref[0], (1, reps))
    s1t = jnp.tile(s1_ref[0], (1, reps))
    s2t = jnp.tile(s2_ref[0], (1, reps))

    def rope(u):
        up = pltpu.roll(u, ATTN_WIDTH - ROT_HALF, axis=1)
        um = pltpu.roll(u, ROT_HALF, axis=1)
        return u * ct + up * s1t + um * s2t

    q = (rope(proj(OFF_Q, ATTN_WIDTH)) * (HEAD_DIM ** -0.5)).astype(BF16)
    k = rope(proj(OFF_K, ATTN_WIDTH)).astype(BF16)
    v = proj(OFF_V, ATTN_WIDTH).astype(BF16)
    for g, (qr, kr, vr) in enumerate(((q0_ref, k0_ref, v0_ref), (q1_ref, k1_ref, v1_ref), (q2_ref, k2_ref, v2_ref))):
        lo = g * GROUP_WIDTH
        qr[0] = q[:, lo:lo + GROUP_WIDTH]
        kr[0] = k[:, lo:lo + GROUP_WIDTH]
        vr[0] = v[:, lo:lo + GROUP_WIDTH]


def _mix_in(x, g, w_in, pw_bd, pscale, conv_w, wa, wb, rope_c, rope_s1, rope_s2, tm):
    b, s, d = x.shape
    row = lambda width: pl.BlockSpec((1, tm, width), lambda i, j: (i, j, 0))
    qkv_shape = jax.ShapeDtypeStruct((b, s, GROUP_WIDTH), BF16)
    return pl.pallas_call(
        _mix_in_kernel,
        grid=(b, s // tm),
        in_specs=[row(d), _const_spec((1, d)), _const_spec((d, IN_WIDTH)), _const_spec((POOL_WIDTH, POOL_WIDTH)),
                  _const_spec((1, POOL_WIDTH)), _const_spec((3, CONV_WIDTH)), _const_spec((POOL_WIDTH, d)),
                  _const_spec((CONV_WIDTH, d)), row(LANES), row(LANES), row(LANES)],
        out_specs=[row(d), row(d)] + [row(GROUP_WIDTH)] * 9,
        out_shape=[jax.ShapeDtypeStruct((b, s, d), BF16)] * 2 + [qkv_shape] * 9,
        scratch_shapes=[pltpu.VMEM((POOL_HALO, POOL_WIDTH), F32), pltpu.VMEM((CONV_HALO, CONV_WIDTH), F32)],
        compiler_params=pltpu.CompilerParams(dimension_semantics=("arbitrary", "arbitrary"),
                                             vmem_limit_bytes=VMEM_LIMIT),
        name="mix_in",
    )(x, g, w_in, pw_bd, pscale, conv_w, wa, wb, rope_c, rope_s1, rope_s2)


def _attn_kernel(q_ref, k_ref, v_ref, kp_ref, vp_ref, o_ref, lse_ref, kbuf, vbuf, *, dil):
    rs = q_ref.shape[1]
    nqb = rs // Q_BLOCK
    i = pl.program_id(1)
    kbuf[0:Q_BLOCK] = kp_ref[0]
    kbuf[Q_BLOCK:] = k_ref[0]
    vbuf[0:Q_BLOCK] = vp_ref[0]
    vbuf[Q_BLOCK:] = v_ref[0]

    nk = 2 * Q_BLOCK
    row = lax.broadcasted_iota(jnp.int32, (Q_BLOCK, nk), 0)
    col = lax.broadcasted_iota(jnp.int32, (Q_BLOCK, nk), 1)
    back = row - col + Q_BLOCK
    band = (back >= 0) & (back <= WINDOW_KEYS)
    lane = lax.broadcasted_iota(jnp.int32, (1, GROUP_WIDTH), 1)
    head_of_lane = lane // HEAD_DIM

    def q_block(qb, r):
        c0 = r * GROUP_WIDTH
        start = pl.multiple_of(qb * Q_BLOCK, Q_BLOCK)
        first = (i * nqb + qb) == 0
        valid = band & (col >= jnp.where(first, Q_BLOCK, 0))
        bias = jnp.where(valid, 0.0, NEG_BIG)
        q4 = q_ref[0, pl.ds(start, Q_BLOCK), c0:c0 + GROUP_WIDTH]
        k2 = kbuf[pl.ds(start, nk), c0:c0 + GROUP_WIDTH]
        v2 = vbuf[pl.ds(start, nk), c0:c0 + GROUP_WIDTH]
        o_acc = jnp.zeros((Q_BLOCK, GROUP_WIDTH), F32)
        lse_acc = jnp.zeros((Q_BLOCK, GROUP_WIDTH), F32)
        for hh in range(HEADS_PER_GROUP):
            hm = head_of_lane == hh
            qh = jnp.where(hm, q4, jnp.zeros_like(q4))
            sc = lax.dot_general(qh, k2, (((1,), (1,)), ((), ())), preferred_element_type=F32) + bias
            m = jnp.max(sc, axis=-1, keepdims=True)
            p = jnp.exp(sc - m)
            den = jnp.sum(p, axis=-1, keepdims=True)
            of = _dot(p.astype(BF16), v2)
            o_acc = jnp.where(hm, of / den, o_acc)
            lse_acc = jnp.where(hm, m + jnp.log(den), lse_acc)
        o_ref[0, pl.ds(start, Q_BLOCK), c0:c0 + GROUP_WIDTH] = o_acc.astype(o_ref.dtype)
        lse_ref[0, pl.ds(start, Q_BLOCK), c0:c0 + GROUP_WIDTH] = lse_acc

    for r in range(dil):
        if nqb == 1:
            q_block(0, r)
        else:
            lax.fori_loop(0, nqb, lambda qb, c, r=r: (q_block(qb, r), c)[1], 0)


def _attention_group(q, k, v, dil):
    b, s, w = q.shape
    sub = s // dil
    rs = ATTN_CHUNK // dil
    view = lambda u: u.reshape(b, sub, dil * w)
    cur = pl.BlockSpec((1, rs, dil * w), lambda bi, i: (bi, i, 0))
    hb = rs // Q_BLOCK
    prev = pl.BlockSpec((1, Q_BLOCK, dil * w), lambda bi, i: (bi, jnp.maximum(i * hb - 1, 0), 0))
    o, lse = pl.pallas_call(
        functools.partial(_attn_kernel, dil=dil),
        grid=(b, sub // rs),
        in_specs=[cur, cur, cur, prev, prev],
        out_specs=[cur, cur],
        out_shape=[jax.ShapeDtypeStruct((b, sub, dil * w), BF16), jax.ShapeDtypeStruct((b, sub, dil * w), F32)],
        scratch_shapes=[pltpu.VMEM((rs + Q_BLOCK, dil * w), BF16), pltpu.VMEM((rs + Q_BLOCK, dil * w), BF16)],
        compiler_params=pltpu.CompilerParams(dimension_semantics=("arbitrary", "arbitrary"),
                                             vmem_limit_bytes=VMEM_LIMIT),
        name=f"attn_d{dil}",
    )(view(q), view(k), view(v), view(k), view(v))
    return o.reshape(b, s, w), lse.reshape(b, s, w)


def _mem_kv_kernel(mem_ref, g_ref, w_ref, kt_ref, v_ref):
    mn = _rms(mem_ref[0], g_ref[...]).astype(BF16)
    kv = _dot(mn, w_ref[...])
    kt_ref[0] = kv[:, :MEM_WIDTH].T.astype(BF16)
    v_ref[0] = kv[:, MEM_WIDTH:].astype(BF16)


def _mem_kv(mem, g, w_kv):
    b, m, d = mem.shape
    return pl.pallas_call(
        _mem_kv_kernel,
        grid=(b,),
        in_specs=[pl.BlockSpec((1, m, d), lambda i: (i, 0, 0)), _const_spec((1, d)), _const_spec((d, 2 * MEM_WIDTH))],
        out_specs=[pl.BlockSpec((1, MEM_WIDTH, m), lambda i: (i, 0, 0)), pl.BlockSpec((1, m, MEM_WIDTH), lambda i: (i, 0, 0))],
        out_shape=[jax.ShapeDtypeStruct((b, MEM_WIDTH, m), BF16), jax.ShapeDtypeStruct((b, m, MEM_WIDTH), BF16)],
        compiler_params=pltpu.CompilerParams(dimension_semantics=("arbitrary",), vmem_limit_bytes=VMEM_LIMIT),
        name="mem_kv",
    )(mem, g, w_kv)


def _mix_out_kernel(x_ref, part_ref, g2_ref, o0_ref, o1_ref, o2_ref, l0_ref, l1_ref, l2_ref,
                    wc_ref, wo_ref, gmix_ref, gmpre_ref, wmq_ref, kt_ref, mv_ref, wmo_ref, gmpost_ref,
                    gfpre_ref, wup_ref, cf_ref, wdn_ref, gfpost_ref, out_ref, halo_u, act_ref):
    tm = x_ref.shape[1]
    j = pl.program_id(1)

    @pl.when(j == 0)
    def _():
        halo_u[...] = jnp.zeros_like(halo_u)

    l0, l1, l2 = l0_ref[0], l1_ref[0], l2_ref[0]
    lm = jnp.maximum(jnp.maximum(l0, l1), l2)
    e0, e1, e2 = jnp.exp(l0 - lm), jnp.exp(l1 - lm), jnp.exp(l2 - lm)
    attn = (e0 * o0_ref[0].astype(F32) + e1 * o1_ref[0].astype(F32) + e2 * o2_ref[0].astype(F32)) / (e0 + e1 + e2)
    br_c = _dot(attn.astype(BF16), wc_ref[...])
    merged = part_ref[0].astype(F32) + g2_ref[0].astype(F32) * br_c
    x = x_ref[0] + _rms(_dot(merged.astype(BF16), wo_ref[...]), gmix_ref[...])

    h = _rms(x, gmpre_ref[...]).astype(BF16)
    qm = (_dot(h, wmq_ref[...]) * (MEM_HEAD_DIM ** -0.5)).astype(BF16)
    heads = []
    for hh in range(MEM_HEADS):
        lo = hh * MEM_HEAD_DIM
        sc = _dot(qm[:, lo:lo + MEM_HEAD_DIM], kt_ref[0, lo:lo + MEM_HEAD_DIM, :])
        m = jnp.max(sc, axis=-1, keepdims=True)
        p = jnp.exp(sc - m)
        den = jnp.sum(p, axis=-1, keepdims=True)
        heads.append((_dot(p.astype(BF16), mv_ref[0, :, lo:lo + MEM_HEAD_DIM]) / den).astype(BF16))
    om = jnp.concatenate(heads, axis=1)
    x = x + _rms(_dot(om, wmo_ref[...]), gmpost_ref[...])

    h = _rms(x, gfpre_ref[...]).astype(BF16)
    cf = cf_ref[...]
    for c in range(D_FF // FF_CHUNK):
        lo = c * FF_CHUNK
        ua = _dot(h, wup_ref[:, lo:lo + FF_CHUNK])
        ub = _dot(h, wup_ref[:, D_FF + lo:D_FF + lo + FF_CHUNK])
        hu = halo_u[:, lo:lo + FF_CHUNK]
        cfc = cf[:, lo:lo + FF_CHUNK]
        conv = cfc[2:3] * ua + cfc[1:2] * _shift_rows(ua, hu, 1) + cfc[0:1] * _shift_rows(ua, hu, 2)
        halo_u[:, lo:lo + FF_CHUNK] = ua[tm - CONV_HALO:]
        act_ref[:, lo:lo + FF_CHUNK] = (jax.nn.silu(conv) * ub).astype(BF16)
    y = _dot(act_ref[...], wdn_ref[...])
    out_ref[0] = x + _rms(y, gfpost_ref[...])


def _mix_out(x, part, g2, os_, ls_, wc, wo, gmix, gmpre, wmq, kt, mv, wmo, gmpost, gfpre, wup, cf, wdn, gfpost, tm):
    b, s, d = x.shape
    m = kt.shape[2]
    row = lambda width: pl.BlockSpec((1, tm, width), lambda i, j: (i, j, 0))
    vec = _const_spec((1, d))
    return pl.pallas_call(
        _mix_out_kernel,
        grid=(b, s // tm),
        in_specs=[row(d), row(d), row(d)] + [row(GROUP_WIDTH)] * 6
                 + [_const_spec((GROUP_WIDTH, d)), _const_spec((d, d)), vec, vec, _const_spec((d, MEM_WIDTH)),
                    pl.BlockSpec((1, MEM_WIDTH, m), lambda i, j: (i, 0, 0)),
                    pl.BlockSpec((1, m, MEM_WIDTH), lambda i, j: (i, 0, 0)),
                    _const_spec((MEM_WIDTH, d)), vec, vec, _const_spec((d, 2 * D_FF)), _const_spec((3, D_FF)),
                    _const_spec((D_FF, d)), vec],
        out_specs=row(d),
        out_shape=jax.ShapeDtypeStruct((b, s, d), F32),
        scratch_shapes=[pltpu.VMEM((CONV_HALO, D_FF), F32), pltpu.VMEM((tm, D_FF), BF16)],
        compiler_params=pltpu.CompilerParams(dimension_semantics=("arbitrary", "arbitrary"),
                                             vmem_limit_bytes=VMEM_LIMIT),
        name="mix_out",
    )(x, part, g2, *os_, *ls_, wc, wo, gmix, gmpre, wmq, kt, mv, wmo, gmpost, gfpre, wup, cf, wdn, gfpost)


def _block_diag(w):
    g, n, _ = w.shape
    eye = jnp.eye(g, dtype=w.dtype)
    return (eye[:, None, :, None] * w[:, :, None, :]).reshape(g * n, g * n)


def kernel(x, mem, positions, norm_mix_pre, norm_mix_post, w_in, pool_w, pool_scale, conv_b_w, w_branch_a, w_branch_b, w_branch_c, w_out, norm_mem_pre, norm_mem_post, norm_memkv, w_mq, w_mkv, w_mo, norm_ffn_pre, norm_ffn_post, w_up, conv_ffn_w, w_down):
    b, s, d = x.shape
    depth = w_in.shape[0]
    assert d == D_MODEL and s % ATTN_CHUNK == 0
    tm = ROW_TILE
    rope_c, rope_s1, rope_s2 = (u.reshape(b, s, LANES) for u in _rope_tables(positions))
    vec = lambda p: p.reshape(1, -1)
    bf = lambda p: p.astype(BF16)
    for l in range(depth):
        part, g2, q0, q1, q2, k0, k1, k2, v0, v1, v2 = _mix_in(
            x, vec(norm_mix_pre[l]), bf(w_in[l]), bf(_block_diag(pool_w[l])), vec(pool_scale[l]), conv_b_w[l],
            bf(w_branch_a[l]), bf(w_branch_b[l]), rope_c, rope_s1, rope_s2, tm)
        outs = [_attention_group(qg, kg, vg, dil)
                for qg, kg, vg, dil in ((q0, k0, v0, DILATIONS[0]), (q1, k1, v1, DILATIONS[1]), (q2, k2, v2, DILATIONS[2]))]
        kt, mv = _mem_kv(mem, vec(norm_memkv[l]), bf(w_mkv[l]))
        x = _mix_out(x, part, g2, [o for o, _ in outs], [ls for _, ls in outs],
                     bf(w_branch_c[l]), bf(w_out[l]), vec(norm_mix_post[l]), vec(norm_mem_pre[l]), bf(w_mq[l]),
                     kt, mv, bf(w_mo[l]), vec(norm_mem_post[l]), vec(norm_ffn_pre[l]), bf(w_up[l]),
                     conv_ffn_w[l], bf(w_down[l]), vec(norm_ffn_post[l]), tm)
    return x
```

```python
import math

import jax
import jax.numpy as jnp
from jax import lax
from jax.experimental import pallas as pl
from jax.experimental.pallas import tpu as pltpu

F32 = jnp.float32
BF16 = jnp.bfloat16

D_MODEL = 1024
POOL_GROUPS = 4
POOL_WINDOWS = (2, 4, 8, 16)
POOL_WIDTH = 384
POOL_GROUP_DIM = POOL_WIDTH // POOL_GROUPS
POOL_HALO = 16
CONV_WIDTH = 384
CONV_HALO = 8
HEAD_DIM = 64
DILATIONS = (1, 4, 16)
WINDOW_KEYS = 128
HEADS_PER_GROUP = 4
GROUP_WIDTH = HEADS_PER_GROUP * HEAD_DIM
ATTN_WIDTH = len(DILATIONS) * GROUP_WIDTH
ROT_DIM = 16
ROT_HALF = ROT_DIM // 2
ROPE_THETA = 500000.0
MEM_HEADS = 4
MEM_HEAD_DIM = 128
MEM_WIDTH = MEM_HEADS * MEM_HEAD_DIM
D_FF = 2816
FF_CHUNK = 256
RMS_EPS = 1e-6
NEG_BIG = -1e30

OFF_A = 0
OFF_BX = OFF_A + POOL_WIDTH
OFF_BB = OFF_BX + CONV_WIDTH
OFF_BC = OFF_BB + CONV_WIDTH
OFF_Q = OFF_BC + CONV_WIDTH
OFF_K = OFF_Q + ATTN_WIDTH
OFF_V = OFF_K + ATTN_WIDTH
OFF_G = OFF_V + ATTN_WIDTH
IN_WIDTH = OFF_G + 3 * D_MODEL

LANES = 128
HALVES = GROUP_WIDTH // LANES
ROW_TILE = 512
Q_BLOCK = 128
ATTN_UNROLL = 8
MIX_OUT_SPLIT = 2
VMEM_LIMIT = 56 * 1024 * 1024


def _rms(x, g):
    ms = jnp.mean(x * x, axis=-1, keepdims=True)
    return x * lax.rsqrt(ms + RMS_EPS) * g


def _dot(a, b):
    return jnp.dot(a, b, preferred_element_type=F32)


def _shift_rows(cur, halo, k):
    hr = halo.shape[0]
    ext = jnp.concatenate([halo, cur], axis=0)
    return pltpu.roll(ext, k, axis=0)[hr:]


def _const_spec(shape):
    nd = len(shape)
    return pl.BlockSpec(shape, lambda *_: (0,) * nd, pipeline_mode=pl.Buffered(1))


def _residue_spec(dil, tm):
    return pl.BlockSpec((1, dil, tm // dil, GROUP_WIDTH), lambda i, j: (i, 0, j, 0))


def _rope_kernel(pos_ref, inv_ref, cos_ref, sin_ref, nsin_ref):
    ang = pos_ref[...].astype(F32) * inv_ref[...]
    s = jnp.sin(ang)
    cos_ref[...] = jnp.cos(ang)
    sin_ref[...] = s
    nsin_ref[...] = -s


def _rope_tables(positions):
    t = positions.size
    pos_rep = jnp.repeat(positions.reshape(t), ROT_HALF).reshape(t * ROT_HALF // LANES, LANES)
    inv = ROPE_THETA ** (-jnp.arange(0, ROT_DIM, 2, dtype=F32) / ROT_DIM)
    inv_row = jnp.tile(inv, LANES // ROT_HALF).reshape(1, LANES)
    rows = pos_rep.shape[0]
    rb = min(rows, 512)
    out = jax.ShapeDtypeStruct((rows, LANES), F32)
    cos, sin, nsin = pl.pallas_call(
        _rope_kernel,
        grid=(rows // rb,),
        in_specs=[pl.BlockSpec((rb, LANES), lambda i: (i, 0)), pl.BlockSpec((1, LANES), lambda i: (0, 0))],
        out_specs=[pl.BlockSpec((rb, LANES), lambda i: (i, 0))] * 3,
        out_shape=[out] * 3,
        name="rope_tables",
    )(pos_rep, inv_row)
    cos, sin, nsin = (u.reshape(t, ROT_HALF) for u in (cos, sin, nsin))
    one = jnp.ones((t, HEAD_DIM - ROT_DIM), F32)
    zero = jnp.zeros((t, HEAD_DIM - ROT_DIM), F32)
    z8 = jnp.zeros((t, ROT_HALF), F32)
    reps = LANES // HEAD_DIM
    c = jnp.tile(jnp.concatenate([cos, cos, one], axis=1), (1, reps))
    s1 = jnp.tile(jnp.concatenate([nsin, z8, zero], axis=1), (1, reps))
    s2 = jnp.tile(jnp.concatenate([z8, sin, zero], axis=1), (1, reps))
    return c, s1, s2


def _mix_in_kernel(x_ref, g_ref, w_in_ref, pw_ref, ps_ref, cw_ref, wa_ref, wb_ref, c_ref, s1_ref, s2_ref,
                   part_ref, g2_ref, q0_ref, q1_ref, q2_ref, k0_ref, k1_ref, k2_ref, v0_ref, v1_ref, v2_ref,
                   halo_a, halo_c, perm):
    tm = x_ref.shape[1]
    j = pl.program_id(1)

    @pl.when(j == 0)
    def _():
        halo_a[...] = jnp.zeros_like(halo_a)
        halo_c[...] = jnp.zeros_like(halo_c)

    h = _rms(x_ref[0], g_ref[...]).astype(BF16)

    def proj(lo, width):
        return _dot(h, w_in_ref[:, lo:lo + width])

    small = proj(OFF_A, OFF_Q - OFF_A)
    a = small[:, OFF_A:OFF_A + POOL_WIDTH]
    b_x = small[:, OFF_BX:OFF_BX + CONV_WIDTH]
    b_b = small[:, OFF_BB:OFF_BB + CONV_WIDTH]
    b_c = small[:, OFF_BC:OFF_BC + CONV_WIDTH]

    ext = jnp.concatenate([halo_a[...], a], axis=0)
    s2w = ext + pltpu.roll(ext, 1, axis=0)
    s4w = s2w + pltpu.roll(s2w, 2, axis=0)
    s8w = s4w + pltpu.roll(s4w, 4, axis=0)
    s16w = s8w + pltpu.roll(s8w, 8, axis=0)
    col = lax.broadcasted_iota(jnp.int32, (tm, POOL_WIDTH), 1)
    row = lax.broadcasted_iota(jnp.int32, (tm, POOL_WIDTH), 0) + j * tm
    g0, g1, g2 = (col < POOL_GROUP_DIM), (col < 2 * POOL_GROUP_DIM), (col < 3 * POOL_GROUP_DIM)
    wsum = jnp.where(g0, s2w[POOL_HALO:], jnp.where(g1, s4w[POOL_HALO:], jnp.where(g2, s8w[POOL_HALO:], s16w[POOL_HALO:])))
    win = jnp.where(g0, POOL_WINDOWS[0], jnp.where(g1, POOL_WINDOWS[1], jnp.where(g2, POOL_WINDOWS[2], POOL_WINDOWS[3])))
    cnt = jnp.minimum(row + 1, win).astype(F32)
    pooled = wsum / cnt - a
    halo_a[...] = a[tm - POOL_HALO:]
    mixed = _dot(pooled.astype(BF16), pw_ref[...]) * ps_ref[...]
    br_a = _dot(mixed.astype(BF16), wa_ref[...])

    cx = b_c * b_x
    hc = halo_c[...]
    cw = cw_ref[...]
    conv = cw[2:3] * cx + cw[1:2] * _shift_rows(cx, hc, 1) + cw[0:1] * _shift_rows(cx, hc, 2)
    halo_c[...] = cx[tm - CONV_HALO:]
    br_b = _dot((b_b * conv).astype(BF16), wb_ref[...])

    gate_a = jax.nn.sigmoid(proj(OFF_G, D_MODEL))
    gate_b = jax.nn.sigmoid(proj(OFF_G + D_MODEL, D_MODEL))
    part_ref[0] = (gate_a * br_a + gate_b * br_b).astype(BF16)
    g2_ref[0] = jax.nn.sigmoid(proj(OFF_G + 2 * D_MODEL, D_MODEL)).astype(BF16)

    reps = ATTN_WIDTH // LANES
    ct = jnp.tile(c_ref[0], (1, reps))
    s1t = jnp.tile(s1_ref[0], (1, reps))
    s2t = jnp.tile(s2_ref[0], (1, reps))

    def rope(u):
        up = pltpu.roll(u, ATTN_WIDTH - ROT_HALF, axis=1)
        um = pltpu.roll(u, ROT_HALF, axis=1)
        return u * ct + up * s1t + um * s2t

    q = rope(proj(OFF_Q, ATTN_WIDTH)) * (HEAD_DIM ** -0.5 * math.log2(math.e))
    k = rope(proj(OFF_K, ATTN_WIDTH))
    v = proj(OFF_V, ATTN_WIDTH)
    outs = ((q, (q0_ref, q1_ref, q2_ref)), (k, (k0_ref, k1_ref, k2_ref)), (v, (v0_ref, v1_ref, v2_ref)))
    for ai, (u, refs) in enumerate(outs):
        for g, (dil, ref) in enumerate(zip(DILATIONS, refs)):
            lo = g * GROUP_WIDTH
            if dil == 1:
                ref[0, 0] = u[:, lo:lo + GROUP_WIDTH].astype(BF16)
                continue
            slot = ai * (len(DILATIONS) - 1) + g - 1
            for c in range(HALVES):
                perm[slot, c] = u[:, lo + c * LANES:lo + (c + 1) * LANES]
            for r in range(dil):
                for c in range(HALVES):
                    ref[0, r, :, c * LANES:(c + 1) * LANES] = perm[slot, c, pl.ds(r, tm // dil, stride=dil), :].astype(BF16)


def _mix_in(x, g, w_in, pw_bd, pscale, conv_w, wa, wb, rope_c, rope_s1, rope_s2, tm):
    b, s, d = x.shape
    row = lambda width: pl.BlockSpec((1, tm, width), lambda i, j: (i, j, 0))
    qkv_specs = [_residue_spec(dil, tm) for dil in DILATIONS] * 3
    qkv_shapes = [jax.ShapeDtypeStruct((b, dil, s // dil, GROUP_WIDTH), BF16) for dil in DILATIONS] * 3
    n_perm = 3 * (len(DILATIONS) - 1)
    return pl.pallas_call(
        _mix_in_kernel,
        grid=(b, s // tm),
        in_specs=[row(d), _const_spec((1, d)), _const_spec((d, IN_WIDTH)), _const_spec((POOL_WIDTH, POOL_WIDTH)),
                  _const_spec((1, POOL_WIDTH)), _const_spec((3, CONV_WIDTH)), _const_spec((POOL_WIDTH, d)),
                  _const_spec((CONV_WIDTH, d)), row(LANES), row(LANES), row(LANES)],
        out_specs=[row(d), row(d)] + qkv_specs,
        out_shape=[jax.ShapeDtypeStruct((b, s, d), BF16)] * 2 + qkv_shapes,
        scratch_shapes=[pltpu.VMEM((POOL_HALO, POOL_WIDTH), F32), pltpu.VMEM((CONV_HALO, CONV_WIDTH), F32),
                        pltpu.VMEM((n_perm, HALVES, tm, LANES), F32)],
        compiler_params=pltpu.CompilerParams(dimension_semantics=("arbitrary", "arbitrary"),
                                             vmem_limit_bytes=VMEM_LIMIT),
        name="mix_in",
    )(x, g, w_in, pw_bd, pscale, conv_w, wa, wb, rope_c, rope_s1, rope_s2)


def _attn_kernel(q_ref, k_ref, v_ref, o_ref, st_ref, bias_ref):
    dil, sub = q_ref.shape[1], q_ref.shape[2]
    nqb = sub // Q_BLOCK
    nk = 2 * Q_BLOCK
    row = lax.broadcasted_iota(jnp.int32, (Q_BLOCK, nk), 0)
    col = lax.broadcasted_iota(jnp.int32, (Q_BLOCK, nk), 1)
    for t, off in enumerate((0, Q_BLOCK)):
        back = row - col + off
        bias_ref[t] = jnp.where((back >= 0) & (back <= WINDOW_KEYS), 0.0, NEG_BIG)
    lane = lax.broadcasted_iota(jnp.int32, (1, GROUP_WIDTH), 1) // HEAD_DIM
    head_masks = [lane == hh for hh in range(HEADS_PER_GROUP)]
    l128 = lax.broadcasted_iota(jnp.int32, (1, LANES), 1)
    ones = jnp.ones((nk, LANES), BF16)

    def q_block(u):
        r = u // nqb
        qb = u % nqb
        qs = pl.multiple_of(qb * Q_BLOCK, Q_BLOCK)
        ks = pl.multiple_of(jnp.maximum(qb - 1, 0) * Q_BLOCK, Q_BLOCK)
        bias = bias_ref[jnp.minimum(qb, 1)]
        q4 = q_ref[0, r, pl.ds(qs, Q_BLOCK), :]
        k2 = k_ref[0, r, pl.ds(ks, nk), :]
        v2 = v_ref[0, r, pl.ds(ks, nk), :]
        qst = jnp.concatenate([jnp.where(hm, q4, jnp.zeros_like(q4)) for hm in head_masks], axis=0)
        sc = lax.dot_general(qst, k2, (((1,), (1,)), ((), ())), preferred_element_type=F32)
        sc = sc + jnp.concatenate([bias] * HEADS_PER_GROUP, axis=0)
        m = jnp.max(sc, axis=-1, keepdims=True)
        pb = jnp.exp2(sc - m).astype(BF16)
        halves_o, halves_s = [], []
        for half in range(HALVES):
            c = slice(half * LANES, (half + 1) * LANES)
            rows = slice(half * 2 * Q_BLOCK, (half + 1) * 2 * Q_BLOCK)
            of = _dot(pb[rows], jnp.concatenate([v2[:, c], ones], axis=1))
            m0, m1 = m[rows][:Q_BLOCK], m[rows][Q_BLOCK:]
            halves_o.append(jnp.where(l128 < HEAD_DIM, of[:Q_BLOCK, :LANES], of[Q_BLOCK:, :LANES]))
            halves_s.append(jnp.where(l128 < 32, m0, jnp.where(l128 < 64, of[:Q_BLOCK, LANES:],
                                      jnp.where(l128 < 96, m1, of[Q_BLOCK:, LANES:]))))
        o_ref[0, r, pl.ds(qs, Q_BLOCK), :] = jnp.concatenate(halves_o, axis=1).astype(o_ref.dtype)
        st_ref[0, r, pl.ds(qs, Q_BLOCK), :] = jnp.concatenate(halves_s, axis=1)

    def body(i, carry):
        for t in range(ATTN_UNROLL):
            q_block(i * ATTN_UNROLL + t)
        return carry

    lax.fori_loop(0, dil * nqb // ATTN_UNROLL, body, 0)


def _attention_group(q, k, v):
    b, dil, sub, w = q.shape
    assert sub >= 2 * Q_BLOCK and (dil * sub // Q_BLOCK) % ATTN_UNROLL == 0
    blk = pl.BlockSpec((1, dil, sub, w), lambda bi: (bi, 0, 0, 0))
    return pl.pallas_call(
        _attn_kernel,
        grid=(b,),
        in_specs=[blk, blk, blk],
        out_specs=[blk, blk],
        out_shape=[jax.ShapeDtypeStruct(q.shape, BF16), jax.ShapeDtypeStruct(q.shape, F32)],
        scratch_shapes=[pltpu.VMEM((2, Q_BLOCK, 2 * Q_BLOCK), F32)],
        compiler_params=pltpu.CompilerParams(dimension_semantics=("arbitrary",), vmem_limit_bytes=VMEM_LIMIT),
        name=f"attn_d{dil}",
    )(q, k, v)


def _mem_kv_kernel(mem_ref, g_ref, w_ref, kt_ref, v_ref):
    mn = _rms(mem_ref[0], g_ref[...]).astype(BF16)
    kv = _dot(mn, w_ref[...])
    kt_ref[0] = kv[:, :MEM_WIDTH].T.astype(BF16)
    v_ref[0] = kv[:, MEM_WIDTH:].astype(BF16)


def _mem_kv(mem, g, w_kv):
    b, m, d = mem.shape
    return pl.pallas_call(
        _mem_kv_kernel,
        grid=(b,),
        in_specs=[pl.BlockSpec((1, m, d), lambda i: (i, 0, 0)), _const_spec((1, d)), _const_spec((d, 2 * MEM_WIDTH))],
        out_specs=[pl.BlockSpec((1, MEM_WIDTH, m), lambda i: (i, 0, 0)), pl.BlockSpec((1, m, MEM_WIDTH), lambda i: (i, 0, 0))],
        out_shape=[jax.ShapeDtypeStruct((b, MEM_WIDTH, m), BF16), jax.ShapeDtypeStruct((b, m, MEM_WIDTH), BF16)],
        compiler_params=pltpu.CompilerParams(dimension_semantics=("arbitrary",), vmem_limit_bytes=VMEM_LIMIT),
        name="mem_kv",
    )(mem, g, w_kv)


def _mix_out_kernel(x_ref, part_ref, g2_ref, o0_ref, o1_ref, o2_ref, t0_ref, t1_ref, t2_ref,
                    wc_ref, wo_ref, gmix_ref, gmpre_ref, wmq_ref, kt_ref, mv_ref, wmo_ref, gmpost_ref,
                    gfpre_ref, wup_ref, cf_ref, wdn_ref, gfpost_ref, out_ref, halo_u, act_ref, perm):
    tm = x_ref.shape[1]
    j = pl.program_id(1)

    @pl.when(j == 0)
    def _():
        halo_u[...] = jnp.zeros_like(halo_u)

    for g, (dil, o_ref, t_ref) in enumerate(zip(DILATIONS, (o0_ref, o1_ref, o2_ref), (t0_ref, t1_ref, t2_ref))):
        if dil == 1:
            continue
        for which, ref in enumerate((o_ref, t_ref)):
            slot = 2 * (g - 1) + which
            for r in range(dil):
                for c in range(HALVES):
                    perm[slot, c, pl.ds(r, tm // dil, stride=dil), :] = ref[0, r, :, c * LANES:(c + 1) * LANES].astype(F32)

    ranges = [_mix_out_rows(slice(si * tm // MIX_OUT_SPLIT, (si + 1) * tm // MIX_OUT_SPLIT), x_ref, part_ref, g2_ref,
                            o0_ref, t0_ref, wc_ref, wo_ref, gmix_ref, gmpre_ref, wmq_ref, kt_ref, mv_ref, wmo_ref,
                            gmpost_ref, gfpre_ref, wup_ref, cf_ref, wdn_ref, gfpost_ref, out_ref, halo_u, act_ref, perm)
              for si in range(MIX_OUT_SPLIT)]
    live = []
    while ranges or live:
        if ranges:
            live.append(ranges.pop(0))
        live = [g for g in live if next(g, "done") != "done"]


def _mix_out_rows(rows, x_ref, part_ref, g2_ref, o0_ref, t0_ref, wc_ref, wo_ref, gmix_ref, gmpre_ref, wmq_ref,
                  kt_ref, mv_ref, wmo_ref, gmpost_ref, gfpre_ref, wup_ref, cf_ref, wdn_ref, gfpost_ref, out_ref,
                  halo_u, act_ref, perm):
    n = rows.stop - rows.start
    l128 = lax.broadcasted_iota(jnp.int32, (1, LANES), 1)
    is_max = (l128 % HEAD_DIM) < 32
    attn_halves = []
    for c in range(HALVES):
        cs = slice(c * LANES, (c + 1) * LANES)
        o_g = [o0_ref[0, 0, rows, cs].astype(F32), perm[0, c, rows], perm[2, c, rows]]
        st_g = [t0_ref[0, 0, rows, cs], perm[1, c, rows], perm[3, c, rows]]
        m_g = [jnp.where(is_max, st, pltpu.roll(st, 32, axis=1)) for st in st_g]
        den_g = [jnp.where(is_max, pltpu.roll(st, LANES - 32, axis=1), st) for st in st_g]
        mm = jnp.maximum(jnp.maximum(m_g[0], m_g[1]), m_g[2])
        w_g = [jnp.exp2(m - mm) for m in m_g]
        num = w_g[0] * o_g[0] + w_g[1] * o_g[1] + w_g[2] * o_g[2]
        den = w_g[0] * den_g[0] + w_g[1] * den_g[1] + w_g[2] * den_g[2]
        attn_halves.append((num / den).astype(BF16))
    br_c = _dot(jnp.concatenate(attn_halves, axis=1), wc_ref[...])
    merged = part_ref[0, rows, :].astype(F32) + g2_ref[0, rows, :].astype(F32) * br_c
    x = x_ref[0, rows, :] + _rms(_dot(merged.astype(BF16), wo_ref[...]), gmix_ref[...])

    yield
    h = _rms(x, gmpre_ref[...]).astype(BF16)
    qm = (_dot(h, wmq_ref[...]) * (MEM_HEAD_DIM ** -0.5)).astype(BF16)
    heads = []
    for hh in range(MEM_HEADS):
        lo = hh * MEM_HEAD_DIM
        sc = _dot(qm[:, lo:lo + MEM_HEAD_DIM], kt_ref[0, lo:lo + MEM_HEAD_DIM, :])
        m = jnp.max(sc, axis=-1, keepdims=True)
        p = jnp.exp(sc - m)
        den = jnp.sum(p, axis=-1, keepdims=True)
        heads.append((_dot(p.astype(BF16), mv_ref[0, :, lo:lo + MEM_HEAD_DIM]) / den).astype(BF16))
    om = jnp.concatenate(heads, axis=1)
    x = x + _rms(_dot(om, wmo_ref[...]), gmpost_ref[...])

    yield
    h = _rms(x, gfpre_ref[...]).astype(BF16)
    cf = cf_ref[...]
    for c in range(D_FF // FF_CHUNK):
        lo = c * FF_CHUNK
        ua = _dot(h, wup_ref[:, lo:lo + FF_CHUNK])
        ub = _dot(h, wup_ref[:, D_FF + lo:D_FF + lo + FF_CHUNK])
        hu = halo_u[:, lo:lo + FF_CHUNK]
        cfc = cf[:, lo:lo + FF_CHUNK]
        conv = cfc[2:3] * ua + cfc[1:2] * _shift_rows(ua, hu, 1) + cfc[0:1] * _shift_rows(ua, hu, 2)
        halo_u[:, lo:lo + FF_CHUNK] = ua[n - CONV_HALO:]
        act_ref[rows, lo:lo + FF_CHUNK] = (jax.nn.silu(conv) * ub).astype(BF16)
    yield
    y = _dot(act_ref[rows, :], wdn_ref[...])
    out_ref[0, rows, :] = x + _rms(y, gfpost_ref[...])


def _mix_out(x, part, g2, os_, ts_, wc, wo, gmix, gmpre, wmq, kt, mv, wmo, gmpost, gfpre, wup, cf, wdn, gfpost, tm):
    b, s, d = x.shape
    m = kt.shape[2]
    row = lambda width: pl.BlockSpec((1, tm, width), lambda i, j: (i, j, 0))
    vec = _const_spec((1, d))
    group_specs = [_residue_spec(dil, tm) for dil in DILATIONS]
    return pl.pallas_call(
        _mix_out_kernel,
        grid=(b, s // tm),
        in_specs=[row(d), row(d), row(d)] + group_specs + group_specs
                 + [_const_spec((GROUP_WIDTH, d)), _const_spec((d, d)), vec, vec, _const_spec((d, MEM_WIDTH)),
                    pl.BlockSpec((1, MEM_WIDTH, m), lambda i, j: (i, 0, 0)),
                    pl.BlockSpec((1, m, MEM_WIDTH), lambda i, j: (i, 0, 0)),
                    _const_spec((MEM_WIDTH, d)), vec, vec, _const_spec((d, 2 * D_FF)), _const_spec((3, D_FF)),
                    _const_spec((D_FF, d)), vec],
        out_specs=row(d),
        out_shape=jax.ShapeDtypeStruct((b, s, d), F32),
        scratch_shapes=[pltpu.VMEM((CONV_HALO, D_FF), F32), pltpu.VMEM((tm, D_FF), BF16),
                        pltpu.VMEM((2 * (len(DILATIONS) - 1), HALVES, tm, LANES), F32)],
        compiler_params=pltpu.CompilerParams(dimension_semantics=("arbitrary", "arbitrary"),
                                             vmem_limit_bytes=VMEM_LIMIT),
        name="mix_out",
    )(x, part, g2, *os_, *ts_, wc, wo, gmix, gmpre, wmq, kt, mv, wmo, gmpost, gfpre, wup, cf, wdn, gfpost)


def _block_diag(w):
    g, n, _ = w.shape
    eye = jnp.eye(g, dtype=w.dtype)
    return (eye[:, None, :, None] * w[:, :, None, :]).reshape(g * n, g * n)


def kernel(x, mem, positions, norm_mix_pre, norm_mix_post, w_in, pool_w, pool_scale, conv_b_w, w_branch_a, w_branch_b, w_branch_c, w_out, norm_mem_pre, norm_mem_post, norm_memkv, w_mq, w_mkv, w_mo, norm_ffn_pre, norm_ffn_post, w_up, conv_ffn_w, w_down):
    b, s, d = x.shape
    depth = w_in.shape[0]
    tm = ROW_TILE
    assert d == D_MODEL and s % tm == 0 and tm % (16 * max(DILATIONS)) == 0
    rope_c, rope_s1, rope_s2 = (u.reshape(b, s, LANES) for u in _rope_tables(positions))
    vec = lambda p: p.reshape(1, -1)
    bf = lambda p: p.astype(BF16)
    for l in range(depth):
        part, g2, q0, q1, q2, k0, k1, k2, v0, v1, v2 = _mix_in(
            x, vec(norm_mix_pre[l]), bf(w_in[l]), bf(_block_diag(pool_w[l])), vec(pool_scale[l]), conv_b_w[l],
            bf(w_branch_a[l]), bf(w_branch_b[l]), rope_c, rope_s1, rope_s2, tm)
        outs = [_attention_group(qg, kg, vg) for qg, kg, vg in ((q0, k0, v0), (q1, k1, v1), (q2, k2, v2))]
        kt, mv = _mem_kv(mem, vec(norm_memkv[l]), bf(w_mkv[l]))
        x = _mix_out(x, part, g2, [o for o, _ in outs], [t for _, t in outs],
                     bf(w_branch_c[l]), bf(w_out[l]), vec(norm_mix_post[l]), vec(norm_mem_pre[l]), bf(w_mq[l]),
                     kt, mv, bf(w_mo[l]), vec(norm_mem_post[l]), vec(norm_ffn_pre[l]), bf(w_up[l]),
                     conv_ffn_w[l], bf(w_down[l]), vec(norm_ffn_post[l]), tm)
    return x
```

```python
import math

import jax
import jax.numpy as jnp
from jax import lax
from jax.experimental import pallas as pl
from jax.experimental.pallas import tpu as pltpu

F32 = jnp.float32
BF16 = jnp.bfloat16

D_MODEL = 1024
POOL_GROUPS = 4
POOL_WINDOWS = (2, 4, 8, 16)
POOL_WIDTH = 384
POOL_GROUP_DIM = POOL_WIDTH // POOL_GROUPS
POOL_HALO = 16
CONV_WIDTH = 384
CONV_HALO = 8
HEAD_DIM = 64
DILATIONS = (1, 4, 16)
WINDOW_KEYS = 128
HEADS_PER_GROUP = 4
GROUP_WIDTH = HEADS_PER_GROUP * HEAD_DIM
ATTN_WIDTH = len(DILATIONS) * GROUP_WIDTH
ROT_DIM = 16
ROT_HALF = ROT_DIM // 2
ROPE_THETA = 500000.0
MEM_HEADS = 4
MEM_HEAD_DIM = 128
MEM_WIDTH = MEM_HEADS * MEM_HEAD_DIM
D_FF = 2816
FF_CHUNK = 256
RMS_EPS = 1e-6
NEG_BIG = -1e30

OFF_A = 0
OFF_BX = OFF_A + POOL_WIDTH
OFF_BB = OFF_BX + CONV_WIDTH
OFF_BC = OFF_BB + CONV_WIDTH
OFF_Q = OFF_BC + CONV_WIDTH
OFF_K = OFF_Q + ATTN_WIDTH
OFF_V = OFF_K + ATTN_WIDTH
OFF_G = OFF_V + ATTN_WIDTH
IN_WIDTH = OFF_G + 3 * D_MODEL

LANES = 128
HALVES = GROUP_WIDTH // LANES
ROW_TILE = 512
ROW_SPLIT = 2
Q_BLOCK = 128
ATTN_UNROLL = 32
VMEM_LIMIT = 56 * 1024 * 1024


def _rms(x, g):
    ms = jnp.mean(x * x, axis=-1, keepdims=True)
    return x * lax.rsqrt(ms + RMS_EPS) * g


def _dot(a, b):
    return jnp.dot(a, b, preferred_element_type=F32)


def _shift_rows(cur, halo, k):
    hr = halo.shape[0]
    ext = jnp.concatenate([halo, cur], axis=0)
    return pltpu.roll(ext, k, axis=0)[hr:]


def _trace_staggered(phased_ranges):
    pending, live = list(phased_ranges), []
    while pending or live:
        if pending:
            live.append(pending.pop(0))
        live = [gen for gen in live if next(gen, "done") != "done"]


def _row_ranges(tm):
    return [slice(i * tm // ROW_SPLIT, (i + 1) * tm // ROW_SPLIT) for i in range(ROW_SPLIT)]


def _layer_spec(shape, layer):
    nd = len(shape)
    return pl.BlockSpec((None,) + tuple(shape), lambda *_: (layer,) + (0,) * nd, pipeline_mode=pl.Buffered(1))


def _residue_spec(dil, tm):
    return pl.BlockSpec((1, dil, tm // dil, GROUP_WIDTH), lambda i, j: (i, 0, j, 0))


def _rope_kernel(pos_ref, inv_ref, sign_ref, cos_ref, sin_ref):
    ang = pos_ref[...].astype(F32) * inv_ref[...]
    cos_ref[...] = jnp.cos(ang)
    sin_ref[...] = jnp.sin(ang) * sign_ref[...]


def _rope_tables(positions):
    b, s = positions.shape
    t = b * s
    per_row = LANES // ROT_DIM
    pos_rep = jnp.repeat(positions.reshape(t), ROT_DIM).reshape(t // per_row, LANES)
    inv = ROPE_THETA ** (-jnp.arange(0, ROT_DIM, 2, dtype=F32) / ROT_DIM)
    inv_row = jnp.tile(inv, LANES // ROT_HALF).reshape(1, LANES)
    sign_row = jnp.tile(jnp.repeat(jnp.array([-1.0, 1.0], F32), ROT_HALF), per_row).reshape(1, LANES)
    rows = pos_rep.shape[0]
    rb = min(rows, 512)
    out = jax.ShapeDtypeStruct((rows, LANES), F32)
    const = pl.BlockSpec((1, LANES), lambda i: (0, 0))
    cos, sin = pl.pallas_call(
        _rope_kernel,
        grid=(rows // rb,),
        in_specs=[pl.BlockSpec((rb, LANES), lambda i: (i, 0)), const, const],
        out_specs=[pl.BlockSpec((rb, LANES), lambda i: (i, 0))] * 2,
        out_shape=[out] * 2,
        name="rope_tables",
    )(pos_rep, inv_row, sign_row)
    widen = lambda u: jnp.pad(u.reshape(b, s, ROT_DIM), ((0, 0), (0, 0), (0, LANES - ROT_DIM)))
    return widen(cos), widen(sin)


def _mix_in_kernel(x_ref, g_ref, w_in_ref, pw_ref, ps_ref, cw_ref, wa_ref, wb_ref, cos_ref, sin_ref,
                   part_ref, g2_ref, q0_ref, q1_ref, q2_ref, k0_ref, k1_ref, k2_ref, v0_ref, v1_ref, v2_ref,
                   halo_a, halo_c, perm):
    tm = x_ref.shape[1]
    j = pl.program_id(1)

    @pl.when(j == 0)
    def _():
        halo_a[...] = jnp.zeros_like(halo_a)
        halo_c[...] = jnp.zeros_like(halo_c)

    def row_range(rows):
        n = rows.stop - rows.start
        h = _rms(x_ref[0, rows, :], g_ref[...]).astype(BF16)

        def proj(lo, width):
            return _dot(h, w_in_ref[:, lo:lo + width])

        l128 = lax.broadcasted_iota(jnp.int32, (1, LANES), 1) % HEAD_DIM
        cos2 = cos_ref[0, rows, :] + pltpu.roll(cos_ref[0, rows, :], HEAD_DIM, axis=1)
        sin2 = sin_ref[0, rows, :] + pltpu.roll(sin_ref[0, rows, :], HEAD_DIM, axis=1)
        reps = ATTN_WIDTH // LANES
        ct = jnp.tile(jnp.where(l128 < ROT_DIM, cos2, 1.0), (1, reps))
        s1t = jnp.tile(jnp.where(l128 < ROT_HALF, sin2, 0.0), (1, reps))
        s2t = jnp.tile(jnp.where((l128 >= ROT_HALF) & (l128 < ROT_DIM), sin2, 0.0), (1, reps))

        def rope(u):
            up = pltpu.roll(u, ATTN_WIDTH - ROT_HALF, axis=1)
            um = pltpu.roll(u, ROT_HALF, axis=1)
            return u * ct + up * s1t + um * s2t

        q = rope(proj(OFF_Q, ATTN_WIDTH)) * (HEAD_DIM ** -0.5 * math.log2(math.e))
        k = rope(proj(OFF_K, ATTN_WIDTH))
        v = proj(OFF_V, ATTN_WIDTH)
        outs = ((q, (q0_ref, q1_ref, q2_ref)), (k, (k0_ref, k1_ref, k2_ref)), (v, (v0_ref, v1_ref, v2_ref)))
        for ai, (u, refs) in enumerate(outs):
            for g, (dil, ref) in enumerate(zip(DILATIONS, refs)):
                lo = g * GROUP_WIDTH
                if dil == 1:
                    ref[0, 0, rows, :] = u[:, lo:lo + GROUP_WIDTH].astype(BF16)
                    continue
                slot = ai * (len(DILATIONS) - 1) + g - 1
                sub = slice(rows.start // dil, rows.stop // dil)
                for c in range(HALVES):
                    perm[slot, c, rows, :] = u[:, lo + c * LANES:lo + (c + 1) * LANES]
                for r in range(dil):
                    for c in range(HALVES):
                        picked = perm[slot, c, pl.ds(rows.start + r, n // dil, stride=dil), :]
                        ref[0, r, sub, c * LANES:(c + 1) * LANES] = picked.astype(BF16)

        yield
        small = proj(OFF_A, OFF_Q - OFF_A)
        a = small[:, OFF_A:OFF_A + POOL_WIDTH]
        b_x = small[:, OFF_BX:OFF_BX + CONV_WIDTH]
        b_b = small[:, OFF_BB:OFF_BB + CONV_WIDTH]
        b_c = small[:, OFF_BC:OFF_BC + CONV_WIDTH]

        ext = jnp.concatenate([halo_a[...], a], axis=0)
        s2w = ext + pltpu.roll(ext, 1, axis=0)
        s4w = s2w + pltpu.roll(s2w, 2, axis=0)
        s8w = s4w + pltpu.roll(s4w, 4, axis=0)
        s16w = s8w + pltpu.roll(s8w, 8, axis=0)
        col = lax.broadcasted_iota(jnp.int32, (n, POOL_WIDTH), 1)
        row = lax.broadcasted_iota(jnp.int32, (n, POOL_WIDTH), 0) + j * tm + rows.start
        g0, g1, g2 = (col < POOL_GROUP_DIM), (col < 2 * POOL_GROUP_DIM), (col < 3 * POOL_GROUP_DIM)
        wsum = jnp.where(g0, s2w[POOL_HALO:], jnp.where(g1, s4w[POOL_HALO:], jnp.where(g2, s8w[POOL_HALO:], s16w[POOL_HALO:])))
        win = jnp.where(g0, POOL_WINDOWS[0], jnp.where(g1, POOL_WINDOWS[1], jnp.where(g2, POOL_WINDOWS[2], POOL_WINDOWS[3])))
        cnt = jnp.minimum(row + 1, win).astype(F32)
        pooled = wsum / cnt - a
        halo_a[...] = a[n - POOL_HALO:]
        mixed = _dot(pooled.astype(BF16), pw_ref[...]) * ps_ref[...]
        br_a = _dot(mixed.astype(BF16), wa_ref[...])

        cx = b_c * b_x
        hc = halo_c[...]
        cw = cw_ref[...]
        conv = cw[2:3] * cx + cw[1:2] * _shift_rows(cx, hc, 1) + cw[0:1] * _shift_rows(cx, hc, 2)
        halo_c[...] = cx[n - CONV_HALO:]
        br_b = _dot((b_b * conv).astype(BF16), wb_ref[...])

        yield
        gate_a = jax.nn.sigmoid(proj(OFF_G, D_MODEL))
        gate_b = jax.nn.sigmoid(proj(OFF_G + D_MODEL, D_MODEL))
        part_ref[0, rows, :] = (gate_a * br_a + gate_b * br_b).astype(BF16)
        g2_ref[0, rows, :] = jax.nn.sigmoid(proj(OFF_G + 2 * D_MODEL, D_MODEL)).astype(BF16)

    _trace_staggered(row_range(rows) for rows in _row_ranges(tm))


def _mix_in(x, layer, g, w_in, pw_bd, pscale, conv_w, wa, wb, rope_cos, rope_sin, tm):
    b, s, d = x.shape
    row = lambda width: pl.BlockSpec((1, tm, width), lambda i, j: (i, j, 0))
    par = lambda *shape: _layer_spec(shape, layer)
    qkv_specs = [_residue_spec(dil, tm) for dil in DILATIONS] * 3
    qkv_shapes = [jax.ShapeDtypeStruct((b, dil, s // dil, GROUP_WIDTH), BF16) for dil in DILATIONS] * 3
    n_perm = 3 * (len(DILATIONS) - 1)
    return pl.pallas_call(
        _mix_in_kernel,
        grid=(b, s // tm),
        in_specs=[row(d), par(1, d), par(d, IN_WIDTH), par(POOL_WIDTH, POOL_WIDTH), par(1, POOL_WIDTH),
                  par(3, CONV_WIDTH), par(POOL_WIDTH, d), par(CONV_WIDTH, d), row(LANES), row(LANES)],
        out_specs=[row(d), row(d)] + qkv_specs,
        out_shape=[jax.ShapeDtypeStruct((b, s, d), BF16)] * 2 + qkv_shapes,
        scratch_shapes=[pltpu.VMEM((POOL_HALO, POOL_WIDTH), F32), pltpu.VMEM((CONV_HALO, CONV_WIDTH), F32),
                        pltpu.VMEM((n_perm, HALVES, tm, LANES), F32)],
        compiler_params=pltpu.CompilerParams(dimension_semantics=("arbitrary", "arbitrary"),
                                             vmem_limit_bytes=VMEM_LIMIT),
        name="mix_in",
    )(x, g, w_in, pw_bd, pscale, conv_w, wa, wb, rope_cos, rope_sin)


def _attn_kernel(q_ref, k_ref, v_ref, o_ref, st_ref, bias_ref):
    dil, sub = q_ref.shape[1], q_ref.shape[2]
    nqb = sub // Q_BLOCK
    nk = 2 * Q_BLOCK
    row = lax.broadcasted_iota(jnp.int32, (Q_BLOCK, nk), 0)
    col = lax.broadcasted_iota(jnp.int32, (Q_BLOCK, nk), 1)
    for t, off in enumerate((0, Q_BLOCK)):
        back = row - col + off
        bias_ref[t] = jnp.where((back >= 0) & (back <= WINDOW_KEYS), 0.0, NEG_BIG)
    lane = lax.broadcasted_iota(jnp.int32, (1, GROUP_WIDTH), 1) // HEAD_DIM
    head_masks = [lane == hh for hh in range(HEADS_PER_GROUP)]
    l128 = lax.broadcasted_iota(jnp.int32, (1, LANES), 1)
    ones = jnp.ones((nk, LANES), BF16)

    def q_block(u):
        r = u // nqb
        qb = u % nqb
        qs = pl.multiple_of(qb * Q_BLOCK, Q_BLOCK)
        ks = pl.multiple_of(jnp.maximum(qb - 1, 0) * Q_BLOCK, Q_BLOCK)
        bias = bias_ref[jnp.minimum(qb, 1)]
        q4 = q_ref[0, r, pl.ds(qs, Q_BLOCK), :]
        k2 = k_ref[0, r, pl.ds(ks, nk), :]
        v2 = v_ref[0, r, pl.ds(ks, nk), :]
        qst = jnp.concatenate([jnp.where(hm, q4, jnp.zeros_like(q4)) for hm in head_masks], axis=0)
        sc = lax.dot_general(qst, k2, (((1,), (1,)), ((), ())), preferred_element_type=F32)
        sc = sc + jnp.concatenate([bias] * HEADS_PER_GROUP, axis=0)
        m = jnp.max(sc, axis=-1, keepdims=True)
        pb = jnp.exp2(sc - m).astype(BF16)
        halves_o, halves_s = [], []
        for half in range(HALVES):
            c = slice(half * LANES, (half + 1) * LANES)
            rows = slice(half * 2 * Q_BLOCK, (half + 1) * 2 * Q_BLOCK)
            of = _dot(pb[rows], jnp.concatenate([v2[:, c], ones], axis=1))
            m0, m1 = m[rows][:Q_BLOCK], m[rows][Q_BLOCK:]
            halves_o.append(jnp.where(l128 < HEAD_DIM, of[:Q_BLOCK, :LANES], of[Q_BLOCK:, :LANES]))
            halves_s.append(jnp.where(l128 < 32, m0, jnp.where(l128 < 64, of[:Q_BLOCK, LANES:],
                                      jnp.where(l128 < 96, m1, of[Q_BLOCK:, LANES:]))))
        o_ref[0, r, pl.ds(qs, Q_BLOCK), :] = jnp.concatenate(halves_o, axis=1).astype(o_ref.dtype)
        st_ref[0, r, pl.ds(qs, Q_BLOCK), :] = jnp.concatenate(halves_s, axis=1)

    def body(i, carry):
        for t in range(ATTN_UNROLL):
            q_block(i * ATTN_UNROLL + t)
        return carry

    lax.fori_loop(0, dil * nqb // ATTN_UNROLL, body, 0)


def _attention_group(q, k, v):
    b, dil, sub, w = q.shape
    assert sub >= 2 * Q_BLOCK and (dil * sub // Q_BLOCK) % ATTN_UNROLL == 0
    blk = pl.BlockSpec((1, dil, sub, w), lambda bi: (bi, 0, 0, 0))
    return pl.pallas_call(
        _attn_kernel,
        grid=(b,),
        in_specs=[blk, blk, blk],
        out_specs=[blk, blk],
        out_shape=[jax.ShapeDtypeStruct(q.shape, BF16), jax.ShapeDtypeStruct(q.shape, F32)],
        scratch_shapes=[pltpu.VMEM((2, Q_BLOCK, 2 * Q_BLOCK), F32)],
        compiler_params=pltpu.CompilerParams(dimension_semantics=("arbitrary",), vmem_limit_bytes=VMEM_LIMIT),
        name=f"attn_d{dil}",
    )(q, k, v)


def _mem_kv_kernel(mem_ref, g_ref, w_ref, kt_ref, v_ref):
    mn = _rms(mem_ref[0], g_ref[...]).astype(BF16)
    kv = _dot(mn, w_ref[...])
    kt_ref[0] = kv[:, :MEM_WIDTH].T.astype(BF16)
    v_ref[0] = kv[:, MEM_WIDTH:].astype(BF16)


def _mem_kv(mem, layer, g, w_kv):
    b, m, d = mem.shape
    return pl.pallas_call(
        _mem_kv_kernel,
        grid=(b,),
        in_specs=[pl.BlockSpec((1, m, d), lambda i: (i, 0, 0)), _layer_spec((1, d), layer),
                  _layer_spec((d, 2 * MEM_WIDTH), layer)],
        out_specs=[pl.BlockSpec((1, MEM_WIDTH, m), lambda i: (i, 0, 0)), pl.BlockSpec((1, m, MEM_WIDTH), lambda i: (i, 0, 0))],
        out_shape=[jax.ShapeDtypeStruct((b, MEM_WIDTH, m), BF16), jax.ShapeDtypeStruct((b, m, MEM_WIDTH), BF16)],
        compiler_params=pltpu.CompilerParams(dimension_semantics=("arbitrary",), vmem_limit_bytes=VMEM_LIMIT),
        name="mem_kv",
    )(mem, g, w_kv)


def _mix_out_kernel(x_ref, part_ref, g2_ref, o0_ref, o1_ref, o2_ref, t0_ref, t1_ref, t2_ref,
                    wc_ref, wo_ref, gmix_ref, gmpre_ref, wmq_ref, kt_ref, mv_ref, wmo_ref, gmpost_ref,
                    gfpre_ref, wup_ref, cf_ref, wdn_ref, gfpost_ref, out_ref, halo_u, act_ref, perm):
    tm = x_ref.shape[1]
    j = pl.program_id(1)

    @pl.when(j == 0)
    def _():
        halo_u[...] = jnp.zeros_like(halo_u)

    for g, (dil, o_ref, t_ref) in enumerate(zip(DILATIONS, (o0_ref, o1_ref, o2_ref), (t0_ref, t1_ref, t2_ref))):
        if dil == 1:
            continue
        for which, ref in enumerate((o_ref, t_ref)):
            slot = 2 * (g - 1) + which
            for r in range(dil):
                for c in range(HALVES):
                    perm[slot, c, pl.ds(r, tm // dil, stride=dil), :] = ref[0, r, :, c * LANES:(c + 1) * LANES].astype(F32)

    def row_range(rows):
        n = rows.stop - rows.start
        l128 = lax.broadcasted_iota(jnp.int32, (1, LANES), 1)
        is_max = (l128 % HEAD_DIM) < 32
        attn_halves = []
        for c in range(HALVES):
            cs = slice(c * LANES, (c + 1) * LANES)
            o_g = [o0_ref[0, 0, rows, cs].astype(F32), perm[0, c, rows], perm[2, c, rows]]
            st_g = [t0_ref[0, 0, rows, cs], perm[1, c, rows], perm[3, c, rows]]
            m_g = [jnp.where(is_max, st, pltpu.roll(st, 32, axis=1)) for st in st_g]
            den_g = [jnp.where(is_max, pltpu.roll(st, LANES - 32, axis=1), st) for st in st_g]
            mm = jnp.maximum(jnp.maximum(m_g[0], m_g[1]), m_g[2])
            w_g = [jnp.exp2(m - mm) for m in m_g]
            num = w_g[0] * o_g[0] + w_g[1] * o_g[1] + w_g[2] * o_g[2]
            den = w_g[0] * den_g[0] + w_g[1] * den_g[1] + w_g[2] * den_g[2]
            attn_halves.append((num / den).astype(BF16))
        br_c = _dot(jnp.concatenate(attn_halves, axis=1), wc_ref[...])
        merged = part_ref[0, rows, :].astype(F32) + g2_ref[0, rows, :].astype(F32) * br_c
        x = x_ref[0, rows, :] + _rms(_dot(merged.astype(BF16), wo_ref[...]), gmix_ref[...])

        yield
        h = _rms(x, gmpre_ref[...]).astype(BF16)
        qm = (_dot(h, wmq_ref[...]) * (MEM_HEAD_DIM ** -0.5)).astype(BF16)
        heads = []
        for hh in range(MEM_HEADS):
            lo = hh * MEM_HEAD_DIM
            sc = _dot(qm[:, lo:lo + MEM_HEAD_DIM], kt_ref[0, lo:lo + MEM_HEAD_DIM, :])
            m = jnp.max(sc, axis=-1, keepdims=True)
            p = jnp.exp(sc - m)
            den = jnp.sum(p, axis=-1, keepdims=True)
            heads.append((_dot(p.astype(BF16), mv_ref[0, :, lo:lo + MEM_HEAD_DIM]) / den).astype(BF16))
        om = jnp.concatenate(heads, axis=1)
        x = x + _rms(_dot(om, wmo_ref[...]), gmpost_ref[...])

        yield
        h = _rms(x, gfpre_ref[...]).astype(BF16)
        cf = cf_ref[...]
        for c in range(D_FF // FF_CHUNK):
            lo = c * FF_CHUNK
            ua = _dot(h, wup_ref[:, lo:lo + FF_CHUNK])
            ub = _dot(h, wup_ref[:, D_FF + lo:D_FF + lo + FF_CHUNK])
            hu = halo_u[:, lo:lo + FF_CHUNK]
            cfc = cf[:, lo:lo + FF_CHUNK]
            conv = cfc[2:3] * ua + cfc[1:2] * _shift_rows(ua, hu, 1) + cfc[0:1] * _shift_rows(ua, hu, 2)
            halo_u[:, lo:lo + FF_CHUNK] = ua[n - CONV_HALO:]
            act_ref[rows, lo:lo + FF_CHUNK] = (jax.nn.silu(conv) * ub).astype(BF16)
        yield
        y = _dot(act_ref[rows, :], wdn_ref[...])
        out_ref[0, rows, :] = x + _rms(y, gfpost_ref[...])

    _trace_staggered(row_range(rows) for rows in _row_ranges(tm))


def _mix_out(x, layer, part, g2, os_, ts_, wc, wo, gmix, gmpre, wmq, kt, mv, wmo, gmpost, gfpre, wup, cf, wdn, gfpost, tm):
    b, s, d = x.shape
    m = kt.shape[2]
    row = lambda width: pl.BlockSpec((1, tm, width), lambda i, j: (i, j, 0))
    par = lambda *shape: _layer_spec(shape, layer)
    vec = par(1, d)
    group_specs = [_residue_spec(dil, tm) for dil in DILATIONS]
    return pl.pallas_call(
        _mix_out_kernel,
        grid=(b, s // tm),
        in_specs=[row(d), row(d), row(d)] + group_specs + group_specs
                 + [par(GROUP_WIDTH, d), par(d, d), vec, vec, par(d, MEM_WIDTH),
                    pl.BlockSpec((1, MEM_WIDTH, m), lambda i, j: (i, 0, 0)),
                    pl.BlockSpec((1, m, MEM_WIDTH), lambda i, j: (i, 0, 0)),
                    par(MEM_WIDTH, d), vec, vec, par(d, 2 * D_FF), par(3, D_FF), par(D_FF, d), vec],
        out_specs=row(d),
        out_shape=jax.ShapeDtypeStruct((b, s, d), F32),
        scratch_shapes=[pltpu.VMEM((CONV_HALO, D_FF), F32), pltpu.VMEM((tm, D_FF), BF16),
                        pltpu.VMEM((2 * (len(DILATIONS) - 1), HALVES, tm, LANES), F32)],
        compiler_params=pltpu.CompilerParams(dimension_semantics=("arbitrary", "arbitrary"),
                                             vmem_limit_bytes=VMEM_LIMIT),
        name="mix_out",
    )(x, part, g2, *os_, *ts_, wc, wo, gmix, gmpre, wmq, kt, mv, wmo, gmpost, gfpre, wup, cf, wdn, gfpost)


def _block_diag(w):
    l, g, n, _ = w.shape
    eye = jnp.eye(g, dtype=w.dtype)
    return (eye[None, :, None, :, None] * w[:, :, :, None, :]).reshape(l, g * n, g * n)


def kernel(x, mem, positions, norm_mix_pre, norm_mix_post, w_in, pool_w, pool_scale, conv_b_w, w_branch_a, w_branch_b, w_branch_c, w_out, norm_mem_pre, norm_mem_post, norm_memkv, w_mq, w_mkv, w_mo, norm_ffn_pre, norm_ffn_post, w_up, conv_ffn_w, w_down):
    b, s, d = x.shape
    depth = w_in.shape[0]
    tm = ROW_TILE
    assert d == D_MODEL and s % tm == 0 and tm % (ROW_SPLIT * 16 * max(DILATIONS)) == 0
    rope_cos, rope_sin = _rope_tables(positions)
    vec = lambda p: p.reshape(depth, 1, -1)
    bf = lambda p: p.astype(BF16)
    g_mix_pre, g_mix_post, g_mem_pre, g_mem_post, g_memkv, g_ffn_pre, g_ffn_post, pscale = map(
        vec, (norm_mix_pre, norm_mix_post, norm_mem_pre, norm_mem_post, norm_memkv, norm_ffn_pre, norm_ffn_post, pool_scale))
    w_in_b, pw_b, wa_b, wb_b, wc_b, wo_b, wmq_b, wmkv_b, wmo_b, wup_b, wdn_b = map(
        bf, (w_in, _block_diag(pool_w), w_branch_a, w_branch_b, w_branch_c, w_out, w_mq, w_mkv, w_mo, w_up, w_down))
    for l in range(depth):
        part, g2, q0, q1, q2, k0, k1, k2, v0, v1, v2 = _mix_in(
            x, l, g_mix_pre, w_in_b, pw_b, pscale, conv_b_w, wa_b, wb_b, rope_cos, rope_sin, tm)
        outs = [_attention_group(qg, kg, vg) for qg, kg, vg in ((q0, k0, v0), (q1, k1, v1), (q2, k2, v2))]
        kt, mv = _mem_kv(mem, l, g_memkv, wmkv_b)
        x = _mix_out(x, l, part, g2, [o for o, _ in outs], [t for _, t in outs], wc_b, wo_b, g_mix_post, g_mem_pre,
                     wmq_b, kt, mv, wmo_b, g_mem_post, g_ffn_pre, wup_b, conv_ffn_w, wdn_b, g_ffn_post, tm)
    return x
```
